```python
import math
import jax, jax.numpy as jnp
from jax import lax
import numpy as np

D_MODEL = 2048
BATCH = 4
SEQ = 2048
DEPTH = 2
DEC_BATCH = 128
DEC_SEQ = 4
PAST_LEN = 2048
PAGE_SIZE = 128

N_BRANCH = 4
D_BRANCH = 3 * D_MODEL // 8
MIX_WIDTH = N_BRANCH * D_BRANCH
A_GROUPS = ((128, 1), (512, 4), (2048, 16))
A_HEADS = 6
A_HPG = A_HEADS // len(A_GROUPS)
A_HEAD_DIM = D_BRANCH // A_HEADS
Q_BLOCK = 128
REL_BUCKETS = 32
REL_MAX_DIST = 2048
B_CHUNK = 128
B_GROUPS = 4
B_GROUP_DIM = D_BRANCH // B_GROUPS
C_WINDOWS = (2, 4, 8, 16)
C_GROUPS = len(C_WINDOWS)
C_GROUP_DIM = D_BRANCH // C_GROUPS
C_PREFIX = max(C_WINDOWS) - 1
MEM_LEN = 256
M_HEADS = 4
M_HEAD_DIM = D_BRANCH // M_HEADS
D_FF = ((8 * D_MODEL // 3 + 127) // 128) * 128
CONV_W = 3
EPS = 1e-6
COL_A = 0
COL_B = COL_A + 3 * D_BRANCH
COL_C = COL_B + 2 * D_BRANCH
COL_M = COL_C + D_BRANCH
COL_G = COL_M + D_BRANCH
IN_COLS = COL_G + N_BRANCH * D_BRANCH

kernel_name = 'hybrid_dilated_gmlp_pool_memory_decoder_step'


def rmsnorm(x, g):
    xf = x.astype(jnp.float32)
    y = xf * lax.rsqrt(jnp.mean(xf * xf, axis=-1, keepdims=True) + EPS)
    return (y * g.astype(jnp.float32)).astype(x.dtype)


def layernorm(x, g):
    xf = x.astype(jnp.float32)
    xc = xf - jnp.mean(xf, axis=-1, keepdims=True)
    y = xc * lax.rsqrt(jnp.mean(xc * xc, axis=-1, keepdims=True) + EPS)
    return (y * g.astype(jnp.float32)).astype(x.dtype)


def rel_bucket(dist):
    exact = REL_BUCKETS // 2
    d = jnp.maximum(dist.astype(jnp.float32), 1.0)
    log_b = exact + (jnp.log(d / exact) / math.log(REL_MAX_DIST / exact) * (REL_BUCKETS - exact)).astype(jnp.int32)
    return jnp.where(dist < exact, dist, jnp.minimum(log_b, REL_BUCKETS - 1))


def dilated_group_attention(q, kv_k, kv_v, q_idx, dist, bias):
    Bn, Nq, H, Dh = q.shape
    qb = Q_BLOCK if Nq % Q_BLOCK == 0 else Nq
    nb = Nq // qb
    q_blocks = q.reshape(Bn, nb, qb, H, Dh).transpose(1, 0, 2, 3, 4)
    idx_blocks = q_idx.reshape(nb, qb)
    scale = Dh ** -0.5
    bias_f = bias.astype(jnp.float32)[None, :, None, :]

    def one_block(args):
        qblk, qi = args
        kidx = qi[:, None] - dist[None, :]
        valid = kidx >= 0
        kidx = jnp.maximum(kidx, 0)
        kg = jnp.take(kv_k, kidx, axis=1)
        vg = jnp.take(kv_v, kidx, axis=1)
        logits = jnp.einsum('bqhd,bqkhd->bhqk', qblk, kg).astype(jnp.float32) * scale + bias_f
        logits = jnp.where(valid[None, None], logits, -1e30)
        lse = jax.nn.logsumexp(logits, axis=-1)
        p = jnp.exp(logits - lse[..., None]).astype(vg.dtype)
        out = jnp.einsum('bhqk,bqkhd->bqhd', p, vg)
        return out, lse

    out, lse = lax.map(one_block, (q_blocks, idx_blocks))
    out = out.transpose(1, 0, 2, 3, 4).reshape(Bn, Nq, H, Dh)
    lse = lse.transpose(1, 2, 0, 3).reshape(Bn, H, Nq)
    return out, lse


def dilated_mixer(qkv, prefixes, rel_bias):
    Bn, N, _ = qkv.shape
    q, k, v = jnp.split(qkv, 3, axis=-1)
    q = q.reshape(Bn, N, A_HEADS, A_HEAD_DIM)
    k = k.reshape(Bn, N, A_HEADS, A_HEAD_DIM)
    v = v.reshape(Bn, N, A_HEADS, A_HEAD_DIM)
    outs, lses, new_rows = [], [], []
    for g, (win, dil) in enumerate(A_GROUPS):
        hs = slice(g * A_HPG, (g + 1) * A_HPG)
        qg, kg, vg = q[:, :, hs], k[:, :, hs], v[:, :, hs]
        pre = prefixes[g]
        kv_k = jnp.concatenate([pre[:, :, 0], kg], axis=1)
        kv_v = jnp.concatenate([pre[:, :, 1], vg], axis=1)
        dist = jnp.arange(win // dil + 1, dtype=jnp.int32) * dil
        bias = rel_bias[rel_bucket(dist)][:, hs].T
        q_idx = pre.shape[1] + jnp.arange(N, dtype=jnp.int32)
        o, lse = dilated_group_attention(qg, kv_k, kv_v, q_idx, dist, bias)
        outs.append(o)
        lses.append(lse)
        keep = min(win, N)
        new_rows.append(jnp.stack([kg, vg], axis=2)[:, N - keep:])
    alpha = jax.nn.softmax(jnp.stack(lses, axis=0), axis=0)
    alpha = alpha.transpose(0, 1, 3, 2)[..., None].astype(qkv.dtype)
    y = jnp.concatenate([outs[g] * alpha[g] for g in range(len(A_GROUPS))], axis=2)
    return y.reshape(Bn, N, D_BRANCH), new_rows


def spatial_gating_mixer(uv, v_gain, w_s, b_s):
    Bn, N, _ = uv.shape
    uv = jax.nn.gelu(uv, approximate=True)
    u, v = jnp.split(uv, 2, axis=-1)
    v = layernorm(v, v_gain)
    pad = (-N) % B_CHUNK
    nc = (N + pad) // B_CHUNK
    vc = jnp.pad(v, ((0, 0), (0, pad), (0, 0))).reshape(Bn, nc, B_CHUNK, B_GROUPS, B_GROUP_DIM)
    w_causal = w_s * jnp.tril(jnp.ones((B_CHUNK, B_CHUNK), w_s.dtype))
    mixed = jnp.einsum('gij,bnjgc->bnigc', w_causal, vc) + b_s.T[None, None, :, :, None]
    mixed = mixed.reshape(Bn, nc * B_CHUNK, D_BRANCH)[:, :N]
    return u * mixed, v


def pooling_mixer(xc, prefix, pos0, w_c, c_scale):
    Bn, N, _ = xc.shape
    ext = jnp.concatenate([prefix, xc], axis=1)
    cs = jnp.pad(jnp.cumsum(ext.astype(jnp.float32), axis=1), ((0, 0), (1, 0), (0, 0)))
    end = cs[:, C_PREFIX + 1:C_PREFIX + 1 + N]
    pos = pos0 + jnp.arange(N, dtype=jnp.int32)
    xf = xc.astype(jnp.float32)
    diffs = []
    for g, w in enumerate(C_WINDOWS):
        cols = slice(g * C_GROUP_DIM, (g + 1) * C_GROUP_DIM)
        s = end[..., cols] - cs[:, C_PREFIX + 1 - w:C_PREFIX + 1 - w + N, cols]
        cnt = jnp.minimum(pos + 1, w).astype(jnp.float32)
        diffs.append(s / cnt[None, :, None] - xf[..., cols])
    d = jnp.stack(diffs, axis=2).astype(xc.dtype)
    y = jnp.einsum('bngc,gcd->bngd', d, w_c).reshape(Bn, N, D_BRANCH) * c_scale
    return y, ext[:, -C_PREFIX:]


def memory_kv(mem, g, w):
    Bn, M, _ = mem.shape
    return (rmsnorm(mem, g) @ w).reshape(Bn, M, 2, M_HEADS, M_HEAD_DIM)


def memory_attention(q, mem_kv):
    Bn, N, _ = q.shape
    qh = q.reshape(Bn, N, M_HEADS, M_HEAD_DIM)
    logits = jnp.einsum('bnhd,bmhd->bhnm', qh, mem_kv[:, :, 0]).astype(jnp.float32) * (M_HEAD_DIM ** -0.5)
    p = jax.nn.softmax(logits, axis=-1).astype(q.dtype)
    return jnp.einsum('bhnm,bmhd->bnhd', p, mem_kv[:, :, 1]).reshape(Bn, N, D_BRANCH)


def conv_ffn(h, conv_prefix, w_up, conv_w, conv_b, w_down):
    N = h.shape[1]
    up = h @ w_up
    ext = jnp.concatenate([conv_prefix, up], axis=1)
    conv = conv_b
    for t in range(CONV_W):
        conv = conv + conv_w[t] * ext[:, t:t + N]
    a, b = jnp.split(conv, 2, axis=-1)
    return (jax.nn.gelu(a, approximate=True) * b) @ w_down, ext[:, -(CONV_W - 1):]


def trunk_layer(x, a_prefixes, mem_kv, pool_prefix, conv_prefix, pos0, rel_bias,
                g_pre_mix, w_in, g_v, w_s, b_s, w_c, c_scale, w_out, g_post_mix,
                g_pre_ffn, w_up, conv_w, conv_b, w_down, g_post_ffn):
    Bn, N, _ = x.shape
    h = rmsnorm(x, g_pre_mix)
    z = h @ w_in
    y_a, a_new = dilated_mixer(z[..., COL_A:COL_B], a_prefixes, rel_bias)
    y_b, v_new = spatial_gating_mixer(z[..., COL_B:COL_C], g_v, w_s, b_s)
    y_c, pool_new = pooling_mixer(z[..., COL_C:COL_M], pool_prefix, pos0, w_c, c_scale)
    y_m = memory_attention(z[..., COL_M:COL_G], mem_kv)
    gates = jax.nn.sigmoid(z[..., COL_G:])
    merged = jnp.concatenate([y_a, y_b, y_c, y_m], axis=-1) * gates
    x = x + rmsnorm(merged @ w_out, g_post_mix)
    f, conv_new = conv_ffn(rmsnorm(x, g_pre_ffn), conv_prefix, w_up, conv_w, conv_b, w_down)
    x = x + rmsnorm(f, g_post_ffn)
    return x, a_new, v_new, pool_new, conv_new


def setup_inputs(seed: int = 0) -> dict:
    key = jax.random.key(seed)
    ks = jax.random.split(key, 32)
    f32 = jnp.float32

    def nrm(k, shape, scale):
        return jax.random.normal(k, shape, f32) * scale

    def gain(k, shape):
        return 1.0 + 0.05 * jax.random.normal(k, shape, f32)

    kvd = (2, A_HPG, A_HEAD_DIM)
    return {
        'x_prompt': nrm(ks[0], (BATCH, SEQ, D_MODEL), 1.0),
        'x_sample': nrm(ks[1], (DEC_BATCH, DEC_SEQ, D_MODEL), 1.0),
        'cache_a_w128': nrm(ks[2], (DEPTH, DEC_BATCH, min(A_GROUPS[0][0], PAST_LEN)) + kvd, 1.0),
        'cache_a_w512': nrm(ks[3], (DEPTH, DEC_BATCH, min(A_GROUPS[1][0], PAST_LEN)) + kvd, 1.0),
        'cache_a_w2048': nrm(ks[4], (DEPTH, DEC_BATCH, min(A_GROUPS[2][0], PAST_LEN)) + kvd, 1.0),
        'cache_mem_kv': nrm(ks[5], (DEPTH, DEC_BATCH, MEM_LEN, 2, M_HEADS, M_HEAD_DIM), 1.0),
        'state_pool': nrm(ks[6], (DEPTH, DEC_BATCH, C_PREFIX, D_BRANCH), 1.0),
        'state_conv': nrm(ks[7], (DEPTH, DEC_BATCH, CONV_W - 1, 2 * D_FF), 1.0),
        'mem_prompt': nrm(ks[8], (BATCH, MEM_LEN, D_MODEL), 1.0),
        'rel_bias': nrm(ks[9], (REL_BUCKETS, A_HEADS), 0.5),
        'norm_pre_mix': gain(ks[10], (DEPTH, D_MODEL)),
        'w_in': nrm(ks[11], (DEPTH, D_MODEL, IN_COLS), D_MODEL ** -0.5),
        'norm_v_b': gain(ks[12], (DEPTH, D_BRANCH)),
        'w_spatial': nrm(ks[13], (DEPTH, B_GROUPS, B_CHUNK, B_CHUNK), B_CHUNK ** -0.5),
        'b_spatial': 1.0 + 0.1 * jax.random.normal(ks[14], (DEPTH, B_GROUPS, B_CHUNK), f32),
        'w_pool': nrm(ks[15], (DEPTH, C_GROUPS, C_GROUP_DIM, C_GROUP_DIM), C_GROUP_DIM ** -0.5),
        'pool_scale': 1.0 + 0.1 * jax.random.normal(ks[16], (DEPTH, D_BRANCH), f32),
        'norm_mem': gain(ks[17], (DEPTH, D_MODEL)),
        'w_mem_kv': nrm(ks[18], (DEPTH, D_MODEL, 2 * D_BRANCH), D_MODEL ** -0.5),
        'w_out': nrm(ks[19], (DEPTH, MIX_WIDTH, D_MODEL), MIX_WIDTH ** -0.5),
        'norm_post_mix': gain(ks[20], (DEPTH, D_MODEL)),
        'norm_pre_ffn': gain(ks[21], (DEPTH, D_MODEL)),
        'w_up': nrm(ks[22], (DEPTH, D_MODEL, 2 * D_FF), D_MODEL ** -0.5),
        'conv_w': nrm(ks[23], (DEPTH, CONV_W, 2 * D_FF), CONV_W ** -0.5),
        'conv_b': nrm(ks[24], (DEPTH, 2 * D_FF), 0.02),
        'w_down': nrm(ks[25], (DEPTH, D_FF, D_MODEL), D_FF ** -0.5),
        'norm_post_ffn': gain(ks[26], (DEPTH, D_MODEL)),
    }


def reference(x_prompt, x_sample, cache_a_w128, cache_a_w512, cache_a_w2048, cache_mem_kv,
              state_pool, state_conv, mem_prompt, rel_bias, norm_pre_mix, w_in, norm_v_b,
              w_spatial, b_spatial, w_pool, pool_scale, norm_mem, w_mem_kv, w_out,
              norm_post_mix, norm_pre_ffn, w_up, conv_w, conv_b, w_down, norm_post_ffn):
    xp, xs = x_prompt, x_sample
    Bp = xp.shape[0]
    a_p = [[], [], []]
    a_s = [[], [], []]
    memkv_p, pool_p, conv_p = [], [], []
    v_s, pool_s, conv_s = [], [], []
    for l in range(DEPTH):
        lw = (rel_bias, norm_pre_mix[l], w_in[l], norm_v_b[l], w_spatial[l], b_spatial[l],
              w_pool[l], pool_scale[l], w_out[l], norm_post_mix[l], norm_pre_ffn[l],
              w_up[l], conv_w[l], conv_b[l], w_down[l], norm_post_ffn[l])
        mkv = memory_kv(mem_prompt, norm_mem[l], w_mem_kv[l])
        empty_a = [jnp.zeros((Bp, 0, 2, A_HPG, A_HEAD_DIM), xp.dtype) for _ in A_GROUPS]
        xp, an, _, pn, cn = trunk_layer(
            xp, empty_a, mkv,
            jnp.zeros((Bp, C_PREFIX, D_BRANCH), xp.dtype),
            jnp.zeros((Bp, CONV_W - 1, 2 * D_FF), xp.dtype), 0, *lw)
        for g in range(len(A_GROUPS)):
            a_p[g].append(an[g])
        memkv_p.append(mkv)
        pool_p.append(pn)
        conv_p.append(cn)
        xs, an, vn, pn, cn = trunk_layer(
            xs, [cache_a_w128[l], cache_a_w512[l], cache_a_w2048[l]], cache_mem_kv[l],
            state_pool[l], state_conv[l], PAST_LEN, *lw)
        for g in range(len(A_GROUPS)):
            a_s[g].append(an[g])
        v_s.append(vn)
        pool_s.append(pn)
        conv_s.append(cn)
    return (xp, xs,
            jnp.stack(a_p[0]), jnp.stack(a_p[1]), jnp.stack(a_p[2]),
            jnp.stack(memkv_p), jnp.stack(pool_p), jnp.stack(conv_p),
            jnp.stack(a_s[0]), jnp.stack(a_s[1]), jnp.stack(a_s[2]),
            jnp.stack(v_s), jnp.stack(pool_s), jnp.stack(conv_s))
```

```python
import functools
import math

import numpy as np
import jax
import jax.numpy as jnp
from jax import lax
from jax.experimental import pallas as pl
from jax.experimental.pallas import tpu as pltpu

F32 = jnp.float32
BF16 = jnp.bfloat16

D_MODEL = 2048
DEPTH = 2
PAST_LEN = 2048
D_BRANCH = 768
MIX_WIDTH = 4 * D_BRANCH
A_GROUPS = ((128, 1), (512, 4), (2048, 16))
A_HPG = 2
A_HEAD_DIM = 128
A_KEYS = 129
REL_BUCKETS = 32
REL_MAX_DIST = 2048
B_CHUNK = 128
B_GROUPS = 4
GROUP_DIM = 192
C_WINDOWS = (2, 4, 8, 16)
C_PREFIX = 15
MEM_LEN = 256
M_HEADS = 4
D_FF = 5504
CONV_W = 3
EPS = 1e-6
IN_COLS = 11 * D_BRANCH
NEG = -1e30

LANES = 128
SUBLANES = 8
VMEM_CAP = 60 * 1024 * 1024
Q_BLOCK = 128
FF_TILE = 512
D_FF_PAD = 5632
FF_STEPS = D_FF_PAD // FF_TILE
SAMPLE_T = 4
SAMPLE_B = 128
T_PAD = 8


def _vmem_limit(block_bytes, scratch_bytes=0, temp_bytes=0):
    est = int(1.25 * (2 * block_bytes + scratch_bytes + temp_bytes)) + (2 << 20)
    return min(max(est, 16 << 20), VMEM_CAP)


def _params(semantics, limit):
    return pltpu.CompilerParams(dimension_semantics=semantics, vmem_limit_bytes=limit)


def _rms(x, gain):
    return x * lax.rsqrt(jnp.mean(x * x, axis=-1, keepdims=True) + EPS) * gain


def _gelu(x):
    return jax.nn.gelu(x, approximate=True)


def _col_group_masks(width):
    col = lax.broadcasted_iota(jnp.int32, (1, width), 1)
    return [(col >= g * GROUP_DIM) & (col < (g + 1) * GROUP_DIM) for g in range(4)]


def _rms_matmul_kernel(x_ref, g_ref, w_ref, o_ref, h_ref):
    @pl.when(pl.program_id(1) == 0)
    def _():
        h_ref[...] = _rms(x_ref[...], g_ref[...]).astype(BF16)

    o_ref[...] = jnp.dot(h_ref[...], w_ref[...], preferred_element_type=F32)


def _rms_matmul(x, gain, w, tm, tn):
    rows, k = x.shape
    n = w.shape[1]
    blocks = tm * k * 4 + k * 4 + k * tn * 2 + tm * tn * 4
    return pl.pallas_call(
        _rms_matmul_kernel,
        grid=(rows // tm, n // tn),
        in_specs=[
            pl.BlockSpec((tm, k), lambda i, j: (i, 0)),
            pl.BlockSpec((1, k), lambda i, j: (0, 0)),
            pl.BlockSpec((k, tn), lambda i, j: (0, j)),
        ],
        out_specs=pl.BlockSpec((tm, tn), lambda i, j: (i, j)),
        out_shape=jax.ShapeDtypeStruct((rows, n), F32),
        scratch_shapes=[pltpu.VMEM((tm, k), BF16)],
        compiler_params=_params(("parallel", "arbitrary"),
                                _vmem_limit(blocks, tm * k * 2, tm * k * 4 + tm * tn * 4)),
        name="rms_matmul",
    )(x, gain.reshape(1, k), w)


def _attn_prompt_kernel(q_ref, kp_ref, kc_ref, vp_ref, vc_ref, bias_ref, o_ref, lse_ref):
    has_prev = pl.program_id(2) > 0
    scale = A_HEAD_DIM ** -0.5
    nt = (((1,), (1,)), ((), ()))
    for h in range(A_HPG):
        sl = slice(h * A_HEAD_DIM, (h + 1) * A_HEAD_DIM)
        q = q_ref[:, sl].astype(BF16)
        s_c = lax.dot_general(q, kc_ref[:, sl].astype(BF16), nt, preferred_element_type=F32)
        s_p = lax.dot_general(q, kp_ref[:, sl].astype(BF16), nt, preferred_element_type=F32)
        s_c = s_c * scale + bias_ref[h, :, Q_BLOCK:]
        s_p = s_p * scale + jnp.where(has_prev, bias_ref[h, :, :Q_BLOCK], NEG)
        m = jnp.maximum(jnp.max(s_c, axis=-1, keepdims=True), jnp.max(s_p, axis=-1, keepdims=True))
        e_c = jnp.exp(s_c - m)
        e_p = jnp.exp(s_p - m)
        den = jnp.sum(e_c, axis=-1, keepdims=True) + jnp.sum(e_p, axis=-1, keepdims=True)
        acc = jnp.dot(e_c.astype(BF16), vc_ref[:, sl].astype(BF16), preferred_element_type=F32)
        acc += jnp.dot(e_p.astype(BF16), vp_ref[:, sl].astype(BF16), preferred_element_type=F32)
        o_ref[:, sl] = acc / den
        lse_ref[:, sl] = jnp.broadcast_to(m + jnp.log(den), (Q_BLOCK, A_HEAD_DIM))


def _attn_prompt(z, bias, g):
    bsz, n, _ = z.shape
    dil = A_GROUPS[g][1]
    sub = n // dil
    nb = sub // Q_BLOCK
    width = A_HPG * A_HEAD_DIM
    per_row = IN_COLS // width
    zv = z.reshape(bsz, sub, dil * IN_COLS)
    qc, kc, vc = g, 3 + g, 6 + g

    def spec(col, prev):
        if prev:
            return pl.BlockSpec((None, Q_BLOCK, width),
                                lambda b, r, i: (b, jnp.maximum(i - 1, 0), r * per_row + col))
        return pl.BlockSpec((None, Q_BLOCK, width), lambda b, r, i: (b, i, r * per_row + col))

    out_spec = pl.BlockSpec((None, Q_BLOCK, width), lambda b, r, i: (b, i, r))
    out_shape = jax.ShapeDtypeStruct((bsz, sub, dil * width), F32)
    blocks = 7 * Q_BLOCK * width * 4 + 2 * Q_BLOCK * 2 * Q_BLOCK * 4
    o, lse = pl.pallas_call(
        _attn_prompt_kernel,
        grid=(bsz, dil, nb),
        in_specs=[spec(qc, False), spec(kc, True), spec(kc, False), spec(vc, True), spec(vc, False),
                  pl.BlockSpec((A_HPG, Q_BLOCK, 2 * Q_BLOCK), lambda b, r, i: (0, 0, 0))],
        out_specs=[out_spec, out_spec],
        out_shape=[out_shape, out_shape],
        compiler_params=_params(("parallel", "parallel", "arbitrary"), _vmem_limit(blocks, 0, 4 << 20)),
        name=f"attn_prompt_g{g}",
    )(zv, zv, zv, zv, zv, bias)
    return o.reshape(bsz, n, width), lse.reshape(bsz, n, width)


def _group_alpha(lses):
    m = jnp.maximum(jnp.maximum(lses[0], lses[1]), lses[2])
    es = [jnp.exp(l - m) for l in lses]
    den = es[0] + es[1] + es[2]
    return [e / den for e in es]


def _layernorm(v, gain):
    vc = v - jnp.mean(v, axis=-1, keepdims=True)
    return vc * lax.rsqrt(jnp.mean(vc * vc, axis=-1, keepdims=True) + EPS) * gain


def _memory_attention(qm, k, v, masks):
    scale = GROUP_DIM ** -0.5
    nt = (((1,), (1,)), ((), ()))
    out = jnp.zeros(qm.shape, F32)
    for h in range(M_HEADS):
        qh = jnp.where(masks[h], qm, 0.0).astype(BF16)
        s = lax.dot_general(qh, k, nt, preferred_element_type=F32) * scale
        m = jnp.max(s, axis=-1, keepdims=True)
        e = jnp.exp(s - m)
        p = e / jnp.sum(e, axis=-1, keepdims=True)
        o = jnp.dot(p.astype(BF16), v, preferred_element_type=F32)
        out = jnp.where(masks[h], o, out)
    return out


def _merge_prompt_kernel(o0_ref, o1_ref, o2_ref, l0_ref, l1_ref, l2_ref,
                         zu_ref, zv_ref, zc_ref, halo_ref, zq_ref,
                         g0_ref, g1_ref, g2_ref, g3_ref, mk_ref, mv_ref,
                         gv_ref, ws_ref, bs_ref, wc_ref, cs_ref,
                         out_ref, ext_ref):
    i = pl.program_id(1)
    masks = _col_group_masks(D_BRANCH)

    alphas = _group_alpha([l0_ref[...], l1_ref[...], l2_ref[...]])
    width = A_HPG * A_HEAD_DIM
    for g, o_ref in enumerate((o0_ref, o1_ref, o2_ref)):
        gate = jax.nn.sigmoid(g0_ref[:, g * width:(g + 1) * width])
        out_ref[:, g * width:(g + 1) * width] = (o_ref[...] * alphas[g] * gate).astype(BF16)

    u = _gelu(zu_ref[...])
    v = _layernorm(_gelu(zv_ref[...]), gv_ref[...]).astype(BF16)
    mixed_all = jnp.dot(ws_ref[...], v, preferred_element_type=F32)
    mixed = jnp.zeros((B_CHUNK, D_BRANCH), F32)
    for g in range(B_GROUPS):
        mixed = jnp.where(masks[g], mixed_all[g * B_CHUNK:(g + 1) * B_CHUNK], mixed)
    y_b = u * (mixed + bs_ref[...])
    out_ref[:, D_BRANCH:2 * D_BRANCH] = (y_b * jax.nn.sigmoid(g1_ref[...])).astype(BF16)

    xc = zc_ref[...]
    halo = 2 * SUBLANES

    @pl.when(i == 0)
    def _():
        ext_ref[0:halo, :] = jnp.zeros((halo, D_BRANCH), F32)

    @pl.when(i > 0)
    def _():
        ext_ref[0:halo, :] = halo_ref[...]

    ext_ref[halo:halo + B_CHUNK, :] = xc
    acc = xc
    win_sum = jnp.zeros((B_CHUNK, D_BRANCH), F32)
    win = jnp.zeros((1, D_BRANCH), F32)
    for k in range(1, max(C_WINDOWS)):
        acc = acc + ext_ref[halo - k:halo - k + B_CHUNK, :]
        if k + 1 in C_WINDOWS:
            g = C_WINDOWS.index(k + 1)
            win_sum = jnp.where(masks[g], acc, win_sum)
            win = jnp.where(masks[g], float(k + 1), win)
    pos = (i * B_CHUNK + lax.broadcasted_iota(jnp.int32, (B_CHUNK, 1), 0)).astype(F32)
    cnt = jnp.minimum(pos + 1.0, win)
    d = (win_sum / cnt - xc).astype(BF16)
    y_c = jnp.dot(d, wc_ref[...], preferred_element_type=F32) * cs_ref[...]
    out_ref[:, 2 * D_BRANCH:3 * D_BRANCH] = (y_c * jax.nn.sigmoid(g2_ref[...])).astype(BF16)

    y_m = _memory_attention(zq_ref[...], mk_ref[...].astype(BF16), mv_ref[...].astype(BF16), masks)
    out_ref[:, 3 * D_BRANCH:4 * D_BRANCH] = (y_m * jax.nn.sigmoid(g3_ref[...])).astype(BF16)


def _merge_prompt(z, attn, mkv, lw):
    bsz, n, _ = z.shape
    nb = n // B_CHUNK
    width = A_HPG * A_HEAD_DIM
    halo = 2 * SUBLANES
    halo_per_chunk = B_CHUNK // halo

    def zspec(col):
        return pl.BlockSpec((None, B_CHUNK, D_BRANCH), lambda b, i: (b, i, col))

    aspec = pl.BlockSpec((None, B_CHUNK, width), lambda b, i: (b, i, 0))
    halo_spec = pl.BlockSpec((None, halo, D_BRANCH),
                             lambda b, i: (b, jnp.maximum(i * halo_per_chunk - 1, 0), 5))

    def const(shape):
        return pl.BlockSpec(shape, lambda b, i: (0,) * len(shape))

    def mspec(col):
        return pl.BlockSpec((None, MEM_LEN, D_BRANCH), lambda b, i: (b, 0, col))

    blocks = (6 * B_CHUNK * width * 4 + 9 * B_CHUNK * D_BRANCH * 4 + 2 * MEM_LEN * D_BRANCH * 4
              + 4 * B_CHUNK * B_CHUNK * 2 + B_CHUNK * D_BRANCH * 4 + D_BRANCH * D_BRANCH * 2
              + B_CHUNK * MIX_WIDTH * 2)
    return pl.pallas_call(
        _merge_prompt_kernel,
        grid=(bsz, nb),
        in_specs=[aspec] * 6 + [zspec(3), zspec(4), zspec(5), halo_spec, zspec(6),
                                zspec(7), zspec(8), zspec(9), zspec(10), mspec(0), mspec(1),
                                const((1, D_BRANCH)), const((B_GROUPS * B_CHUNK, B_CHUNK)),
                                const((B_CHUNK, D_BRANCH)), const((D_BRANCH, D_BRANCH)),
                                const((1, D_BRANCH))],
        out_specs=pl.BlockSpec((None, B_CHUNK, MIX_WIDTH), lambda b, i: (b, i, 0)),
        out_shape=jax.ShapeDtypeStruct((bsz, n, MIX_WIDTH), BF16),
        scratch_shapes=[pltpu.VMEM((halo + B_CHUNK, D_BRANCH), F32)],
        compiler_params=_params(("parallel", "arbitrary"), _vmem_limit(blocks, 1 << 20, 8 << 20)),
        name="merge_prompt",
    )(attn[0][0], attn[1][0], attn[2][0], attn[0][1], attn[1][1], attn[2][1],
      z, z, z, z, z, z, z, z, z, mkv, mkv,
      lw["g_v"], lw["ws_causal"], lw["bs_rows"], lw["wc_blockdiag"], lw["c_scale"])


def _out_proj_kernel(m_ref, w_ref, x_ref, g_ref, o_ref):
    y = jnp.dot(m_ref[...], w_ref[...], preferred_element_type=F32)
    o_ref[...] = x_ref[...] + _rms(y, g_ref[...])


def _out_proj(merged, w_out, x, gain, tm):
    rows, k = merged.shape
    n = w_out.shape[1]
    blocks = tm * k * 2 + k * n * 2 + 2 * tm * n * 4 + n * 4
    return pl.pallas_call(
        _out_proj_kernel,
        grid=(rows // tm,),
        in_specs=[
            pl.BlockSpec((tm, k), lambda i: (i, 0)),
            pl.BlockSpec((k, n), lambda i: (0, 0)),
            pl.BlockSpec((tm, n), lambda i: (i, 0)),
            pl.BlockSpec((1, n), lambda i: (0, 0)),
        ],
        out_specs=pl.BlockSpec((tm, n), lambda i: (i, 0)),
        out_shape=jax.ShapeDtypeStruct((rows, n), F32),
        compiler_params=_params(("parallel",), _vmem_limit(blocks, 0, 2 * tm * n * 4)),
        name="out_proj",
    )(merged, w_out, x, gain.reshape(1, n))


def _ffn_kernel(*refs, tm, shift, prefix_rows, carried, blocks_per_seq):
    if carried:
        (x_ref, gi_ref, wu_ref, cw_ref, cb_ref, wd_ref, go_ref,
         o_ref, tail_ref, h_ref, acc_ref, ext_ref, carry_ref) = refs
    else:
        (x_ref, gi_ref, wu_ref, cw_ref, cb_ref, wd_ref, go_ref, pre_ref,
         o_ref, tail_ref, h_ref, acc_ref, ext_ref) = refs
    i = pl.program_id(0)
    j = pl.program_id(1)

    @pl.when(j == 0)
    def _():
        h_ref[...] = _rms(x_ref[...], gi_ref[...]).astype(BF16)
        acc_ref[...] = jnp.zeros(acc_ref.shape, F32)

    up = jnp.dot(h_ref[...], wu_ref[...], preferred_element_type=F32)
    if carried:
        first = (i % blocks_per_seq) == 0

        @pl.when(first)
        def _():
            ext_ref[0:prefix_rows, :] = jnp.zeros((prefix_rows, 2 * FF_TILE), F32)

        @pl.when(jnp.logical_not(first))
        def _():
            ext_ref[0:prefix_rows, :] = carry_ref[j]

        carry_ref[j] = up[tm - prefix_rows:, :]
    else:
        ext_ref[0:prefix_rows, :] = pre_ref[...]
    ext_ref[prefix_rows:prefix_rows + tm, :] = up
    tail_ref[...] = up[tm - prefix_rows:, :]

    conv = (cb_ref[...]
            + cw_ref[0:1, :] * ext_ref[prefix_rows - 2 * shift:prefix_rows - 2 * shift + tm, :]
            + cw_ref[1:2, :] * ext_ref[prefix_rows - shift:prefix_rows - shift + tm, :]
            + cw_ref[2:3, :] * up)
    act = (_gelu(conv[:, :FF_TILE]) * conv[:, FF_TILE:]).astype(BF16)
    acc_ref[...] += jnp.dot(act, wd_ref[...], preferred_element_type=F32)

    @pl.when(j == pl.num_programs(1) - 1)
    def _():
        o_ref[...] = x_ref[...] + _rms(acc_ref[...], go_ref[...])


def _ffn(x, lw, tm, shift, prefix_rows, blocks_per_seq=1, prefix=None):
    rows, k = x.shape
    carried = prefix is None
    nrb = rows // tm
    tw = 2 * FF_TILE
    in_specs = [
        pl.BlockSpec((tm, k), lambda i, j: (i, 0)),
        pl.BlockSpec((1, k), lambda i, j: (0, 0)),
        pl.BlockSpec((k, tw), lambda i, j: (0, j)),
        pl.BlockSpec((CONV_W, tw), lambda i, j: (0, j)),
        pl.BlockSpec((1, tw), lambda i, j: (0, j)),
        pl.BlockSpec((FF_TILE, k), lambda i, j: (j, 0)),
        pl.BlockSpec((1, k), lambda i, j: (0, 0)),
    ]
    args = [x, lw["g_pre_ffn"], lw["w_up"], lw["conv_w"], lw["conv_b"], lw["w_down"], lw["g_post_ffn"]]
    scratch = [pltpu.VMEM((tm, k), BF16), pltpu.VMEM((tm, k), F32),
               pltpu.VMEM((prefix_rows + tm, tw), F32)]
    if carried:
        scratch.append(pltpu.VMEM((FF_STEPS, prefix_rows, tw), F32))
    else:
        in_specs.append(pl.BlockSpec((prefix_rows, tw), lambda i, j: (0, j)))
        args.append(prefix)
    blocks = (2 * tm * k * 4 + k * tw * 2 + FF_TILE * k * 2 + 5 * tw * 4 + 2 * k * 4
              + 2 * prefix_rows * tw * 4)
    scratch_bytes = tm * k * 6 + (prefix_rows + tm) * tw * 4 + FF_STEPS * prefix_rows * tw * 4
    kern = functools.partial(_ffn_kernel, tm=tm, shift=shift, prefix_rows=prefix_rows,
                             carried=carried, blocks_per_seq=blocks_per_seq)
    return pl.pallas_call(
        kern,
        grid=(nrb, FF_STEPS),
        in_specs=in_specs,
        out_specs=[pl.BlockSpec((tm, k), lambda i, j: (i, 0)),
                   pl.BlockSpec((prefix_rows, tw), lambda i, j: (i, j))],
        out_shape=[jax.ShapeDtypeStruct((rows, k), F32),
                   jax.ShapeDtypeStruct((nrb * prefix_rows, FF_STEPS * tw), F32)],
        scratch_shapes=scratch,
        compiler_params=_params(("arbitrary", "arbitrary"),
                                _vmem_limit(blocks, scratch_bytes, 3 * tm * tw * 4)),
        name="ffn",
    )(*args)


def _attn_sample_kernel(qkv_ref, qm_ref, c0_ref, c1_ref, c2_ref, mem_ref,
                        b0_ref, b1_ref, b2_ref, bn_ref, ya_ref, ym_ref):
    scale = A_HEAD_DIM ** -0.5
    nt = (((1,), (1,)), ((), ()))
    width = A_HPG * A_HEAD_DIM
    caches = (c0_ref[...].astype(BF16), c1_ref[...].astype(BF16),
              c2_ref[...].reshape(-1, LANES).astype(BF16))
    biases = (b0_ref, b1_ref, b2_ref)
    outs, lses = [], []
    for g in range(len(A_GROUPS)):
        def heads(col):
            lo = col * D_BRANCH + g * width
            return jnp.concatenate([qkv_ref[:, lo:lo + A_HEAD_DIM],
                                    qkv_ref[:, lo + A_HEAD_DIM:lo + width]], axis=0).astype(BF16)
        q, kn, vn = heads(0), heads(1), heads(2)
        s_c = lax.dot_general(q, caches[g], nt, preferred_element_type=F32) * scale + biases[g][...]
        s_n = lax.dot_general(q, kn, nt, preferred_element_type=F32) * scale + bn_ref[g]
        m = jnp.maximum(jnp.max(s_c, axis=-1, keepdims=True), jnp.max(s_n, axis=-1, keepdims=True))
        e_c = jnp.exp(s_c - m)
        e_n = jnp.exp(s_n - m)
        den = jnp.sum(e_c, axis=-1, keepdims=True) + jnp.sum(e_n, axis=-1, keepdims=True)
        e_v = pltpu.roll(e_c, 2, 1).astype(BF16)
        acc = jnp.dot(e_v, caches[g], preferred_element_type=F32)
        acc += jnp.dot(e_n.astype(BF16), vn, preferred_element_type=F32)
        outs.append(acc / den)
        lses.append(m + jnp.log(den))
    alphas = _group_alpha(lses)
    for g in range(len(A_GROUPS)):
        y = outs[g] * alphas[g]
        for h in range(A_HPG):
            lo = (g * A_HPG + h) * A_HEAD_DIM
            ya_ref[:, lo:lo + A_HEAD_DIM] = y[h * T_PAD:(h + 1) * T_PAD]

    masks = _col_group_masks(D_BRANCH)
    qm = qm_ref[...]
    qs = jnp.concatenate([jnp.where(masks[h], qm, 0.0) for h in range(M_HEADS)], axis=0).astype(BF16)
    k = mem_ref[:, :D_BRANCH].astype(BF16)
    v = mem_ref[:, D_BRANCH:].astype(BF16)
    s = lax.dot_general(qs, k, nt, preferred_element_type=F32) * (GROUP_DIM ** -0.5)
    e = jnp.exp(s - jnp.max(s, axis=-1, keepdims=True))
    p = e / jnp.sum(e, axis=-1, keepdims=True)
    o = jnp.dot(p.astype(BF16), v, preferred_element_type=F32)
    y_m = jnp.zeros((T_PAD, D_BRANCH), F32)
    for h in range(M_HEADS):
        y_m = jnp.where(masks[h], o[h * T_PAD:(h + 1) * T_PAD], y_m)
    ym_ref[...] = y_m


def _attn_sample(qkv, qm, c0, c1, c2, mem, biases, bias_new):
    bsz = qkv.shape[0]
    rows1 = c1.shape[1]
    kept = c2.shape[2] // (A_GROUPS[2][1] // SAMPLE_T)

    def bspec(shape):
        return pl.BlockSpec((None,) + shape, lambda b: (b,) + (0,) * len(shape))

    def const(shape):
        return pl.BlockSpec(shape, lambda b: (0,) * len(shape))

    blocks = (T_PAD * 4 * D_BRANCH * 4 + (c0.shape[1] + 2 * rows1) * LANES * 4
              + MEM_LEN * 2 * D_BRANCH * 4 + 2 * T_PAD * (c0.shape[1] + 2 * rows1) * 4
              + 2 * T_PAD * D_BRANCH * 4)
    out = jax.ShapeDtypeStruct((bsz, T_PAD, D_BRANCH), F32)
    return pl.pallas_call(
        _attn_sample_kernel,
        grid=(bsz,),
        in_specs=[bspec((T_PAD, 3 * D_BRANCH)), bspec((T_PAD, D_BRANCH)),
                  bspec(c0.shape[1:]), bspec(c1.shape[1:]),
                  pl.BlockSpec((None, c2.shape[1], kept, LANES), lambda b: (b, 0, 0, 0)),
                  bspec((MEM_LEN, 2 * D_BRANCH)),
                  const(biases[0].shape), const(biases[1].shape), const(biases[2].shape),
                  const(bias_new.shape)],
        out_specs=[bspec((T_PAD, D_BRANCH)), bspec((T_PAD, D_BRANCH))],
        out_shape=[out, out],
        compiler_params=_params(("parallel",), _vmem_limit(blocks, 0, 8 << 20)),
        name="attn_sample",
    )(qkv, qm, c0, c1, c2, mem, biases[0], biases[1], biases[2], bias_new)


def _merge_sample_kernel(ya_ref, ym_ref, zu_ref, zv_ref, zc_ref,
                         g0_ref, g1_ref, g2_ref, g3_ref, pool_ref,
                         gv_ref, ws_ref, bs_ref, wc_ref, cs_ref, out_ref, vn_ref):
    masks = _col_group_masks(D_BRANCH)
    win = jnp.zeros((1, D_BRANCH), F32)
    for g, w in enumerate(C_WINDOWS):
        win = jnp.where(masks[g], float(w), win)
    vs = []
    for t in range(SAMPLE_T):
        v = _layernorm(_gelu(zv_ref[t]), gv_ref[...])
        vn_ref[t] = v
        vs.append(v)
    for t in range(SAMPLE_T):
        out_ref[t, :, 0:D_BRANCH] = (ya_ref[t] * jax.nn.sigmoid(g0_ref[t])).astype(BF16)
        out_ref[t, :, 3 * D_BRANCH:] = (ym_ref[t] * jax.nn.sigmoid(g3_ref[t])).astype(BF16)

        mixed = bs_ref[t:t + 1, :]
        for s in range(t + 1):
            mixed = mixed + ws_ref[t * SAMPLE_T + s:t * SAMPLE_T + s + 1, :] * vs[s]
        y_b = _gelu(zu_ref[t]) * mixed
        out_ref[t, :, D_BRANCH:2 * D_BRANCH] = (y_b * jax.nn.sigmoid(g1_ref[t])).astype(BF16)

        xc = zc_ref[t]
        acc = xc
        win_sum = jnp.zeros(xc.shape, F32)
        for k in range(1, max(C_WINDOWS)):
            src = t - k
            acc = acc + (zc_ref[src] if src >= 0 else pool_ref[C_PREFIX + src])
            if k + 1 in C_WINDOWS:
                win_sum = jnp.where(masks[C_WINDOWS.index(k + 1)], acc, win_sum)
        d = (win_sum / win - xc).astype(BF16)
        y_c = jnp.dot(d, wc_ref[...], preferred_element_type=F32) * cs_ref[...]
        out_ref[t, :, 2 * D_BRANCH:3 * D_BRANCH] = (y_c * jax.nn.sigmoid(g2_ref[t])).astype(BF16)


def _merge_sample(z, ya, ym, pool, lw, bb):
    bsz = z.shape[1]

    def tspec(col, rows=SAMPLE_T, width=D_BRANCH):
        return pl.BlockSpec((rows, bb, width), lambda i: (0, i, col))

    def const(shape):
        return pl.BlockSpec(shape, lambda i: (0,) * len(shape))

    blocks = (10 * SAMPLE_T + C_PREFIX) * bb * D_BRANCH * 4 + D_BRANCH * D_BRANCH * 2 \
        + SAMPLE_T * bb * MIX_WIDTH * 2
    return pl.pallas_call(
        _merge_sample_kernel,
        grid=(bsz // bb,),
        in_specs=[tspec(0), tspec(0), tspec(3), tspec(4), tspec(5),
                  tspec(7), tspec(8), tspec(9), tspec(10), tspec(0, rows=C_PREFIX),
                  const((1, D_BRANCH)), const((SAMPLE_T * SAMPLE_T, D_BRANCH)),
                  const((SAMPLE_T, D_BRANCH)), const((D_BRANCH, D_BRANCH)), const((1, D_BRANCH))],
        out_specs=[tspec(0, width=MIX_WIDTH), tspec(0)],
        out_shape=[jax.ShapeDtypeStruct((SAMPLE_T, bsz, MIX_WIDTH), BF16),
                   jax.ShapeDtypeStruct((SAMPLE_T, bsz, D_BRANCH), F32)],
        compiler_params=_params(("parallel",), _vmem_limit(blocks, 0, 8 << 20)),
        name="merge_sample",
    )(ya, ym, z, z, z, z, z, z, z, pool,
      lw["g_v"], lw["ws_sample"], lw["bs_sample"], lw["wc_blockdiag"], lw["c_scale"])


def _rel_bucket(dist):
    exact = REL_BUCKETS // 2
    d = jnp.maximum(dist.astype(F32), 1.0)
    log_b = exact + (jnp.log(d / exact) / math.log(REL_MAX_DIST / exact)
                     * (REL_BUCKETS - exact)).astype(jnp.int32)
    return jnp.where(dist < exact, dist, jnp.minimum(log_b, REL_BUCKETS - 1))


def _bias_tables(rel_bias):
    tabs = []
    for g, (win, dil) in enumerate(A_GROUPS):
        dist = jnp.arange(win // dil + 1, dtype=jnp.int32) * dil
        tabs.append(rel_bias[_rel_bucket(dist)][:, g * A_HPG:(g + 1) * A_HPG].T.astype(F32))
    return tabs


def _masked_table(tab, head, j, valid):
    j = np.where(valid, j, 0)
    return jnp.where(jnp.asarray(valid), tab[jnp.asarray(head), jnp.asarray(j)], NEG)


def _prompt_bias(tab):
    idx = np.arange(Q_BLOCK)[:, None] - np.arange(Q_BLOCK)[None, :]
    idx = np.broadcast_to(idx, (A_HPG, Q_BLOCK, Q_BLOCK))
    head = np.broadcast_to(np.arange(A_HPG)[:, None, None], idx.shape)
    cur = _masked_table(tab, head, idx, idx >= 0)
    prev = _masked_table(tab, head, Q_BLOCK + idx, idx <= 0)
    return jnp.concatenate([prev, cur], axis=-1)


def _sample_bias(tab, g, flat_rows, kept_positions, period_positions):
    win, dil = A_GROUPS[g]
    past = min(win, PAST_LEN)
    f = np.arange(flat_rows)
    per_pos = 2 * A_HPG
    period, s = f // (kept_positions * per_pos), f % (kept_positions * per_pos)
    pos = period * period_positions + s // per_pos
    is_key, head_k = (s // A_HPG) % 2 == 0, s % A_HPG
    row = np.arange(A_HPG * T_PAD)
    head_q, t = row // T_PAD, row % T_PAD
    num = past + t[:, None] - pos[None, :]
    valid = (is_key[None, :] & (head_k[None, :] == head_q[:, None]) & (t[:, None] < SAMPLE_T)
             & (num >= 0) & (num % dil == 0) & (num // dil < A_KEYS))
    head = np.broadcast_to(head_q[:, None], valid.shape)
    return _masked_table(tab, head, num // dil, valid)


def _sample_bias_new(tab, g):
    dil = A_GROUPS[g][1]
    row = np.arange(A_HPG * T_PAD)
    head, t = row // T_PAD, row % T_PAD
    num = t[:, None] - t[None, :]
    valid = ((head[:, None] == head[None, :]) & (t[:, None] < SAMPLE_T) & (t[None, :] < SAMPLE_T)
             & (num >= 0) & (num % dil == 0))
    return _masked_table(tab, np.broadcast_to(head[:, None], valid.shape), num // dil, valid)


def _ff_tiles(a, axis):
    a = jnp.moveaxis(a, axis, -1)
    lead = a.shape[:-1]
    halves = a.reshape(lead + (2, D_FF))
    halves = jnp.pad(halves, [(0, 0)] * len(lead) + [(0, 0), (0, D_FF_PAD - D_FF)])
    tiles = halves.reshape(lead + (2, FF_STEPS, FF_TILE))
    tiles = jnp.swapaxes(tiles, -3, -2).reshape(lead + (2 * D_FF_PAD,))
    return jnp.moveaxis(tiles, -1, axis)


def _ff_untile(a):
    lead = a.shape[:-1]
    tiles = a.reshape(lead + (FF_STEPS, 2, FF_TILE))
    halves = jnp.swapaxes(tiles, -3, -2).reshape(lead + (2, D_FF_PAD))[..., :D_FF]
    return halves.reshape(lead + (2 * D_FF,))


def _layer_weights(l, p):
    ws = p["w_spatial"][l]
    causal = ws * jnp.tril(jnp.ones((B_CHUNK, B_CHUNK), ws.dtype))
    small = ws[:, :SAMPLE_T, :SAMPLE_T] * jnp.tril(jnp.ones((SAMPLE_T, SAMPLE_T), ws.dtype))
    wc = p["w_pool"][l]
    blockdiag = jnp.zeros((D_BRANCH, D_BRANCH), F32)
    for g in range(4):
        blockdiag = blockdiag.at[g * GROUP_DIM:(g + 1) * GROUP_DIM,
                                 g * GROUP_DIM:(g + 1) * GROUP_DIM].set(wc[g])
    return {
        "g_pre_mix": p["norm_pre_mix"][l],
        "w_in": p["w_in"][l].astype(BF16),
        "g_v": p["norm_v_b"][l].reshape(1, D_BRANCH),
        "ws_causal": causal.reshape(B_GROUPS * B_CHUNK, B_CHUNK).astype(BF16),
        "bs_rows": jnp.repeat(p["b_spatial"][l].T, GROUP_DIM, axis=1),
        "ws_sample": jnp.repeat(small.transpose(1, 2, 0).reshape(SAMPLE_T * SAMPLE_T, B_GROUPS),
                                GROUP_DIM, axis=1),
        "bs_sample": jnp.repeat(p["b_spatial"][l][:, :SAMPLE_T].T, GROUP_DIM, axis=1),
        "wc_blockdiag": blockdiag.astype(BF16),
        "c_scale": p["pool_scale"][l].reshape(1, D_BRANCH),
        "g_mem": p["norm_mem"][l],
        "w_mem": p["w_mem_kv"][l].astype(BF16),
        "w_out": p["w_out"][l].astype(BF16),
        "g_post_mix": p["norm_post_mix"][l],
        "g_pre_ffn": p["norm_pre_ffn"][l].reshape(1, D_MODEL),
        "w_up": _ff_tiles(p["w_up"][l], 1).astype(BF16),
        "conv_w": _ff_tiles(p["conv_w"][l], 1),
        "conv_b": _ff_tiles(p["conv_b"][l].reshape(1, 2 * D_FF), 1),
        "w_down": jnp.pad(p["w_down"][l], ((0, D_FF_PAD - D_FF), (0, 0))).astype(BF16),
        "g_post_ffn": p["norm_post_ffn"][l].reshape(1, D_MODEL),
    }


def _kv_rows(z, g):
    width = A_HPG * A_HEAD_DIM
    k = z[..., D_BRANCH + g * width:D_BRANCH + (g + 1) * width]
    v = z[..., 2 * D_BRANCH + g * width:2 * D_BRANCH + (g + 1) * width]
    lead = z.shape[:-1]
    return jnp.stack([k.reshape(lead + (A_HPG, A_HEAD_DIM)), v.reshape(lead + (A_HPG, A_HEAD_DIM))],
                     axis=-3)


def _prompt_layer(x, mem, lw, tabs):
    bsz, n, _ = x.shape
    rows = bsz * n
    tm = 512
    mkv = _rms_matmul(mem.reshape(bsz * MEM_LEN, D_MODEL), lw["g_mem"], lw["w_mem"], tm, 2 * D_BRANCH)
    z = _rms_matmul(x.reshape(rows, D_MODEL), lw["g_pre_mix"], lw["w_in"], tm, IN_COLS // 6)
    z = z.reshape(bsz, n, IN_COLS)
    attn = [_attn_prompt(z, _prompt_bias(tabs[g]), g) for g in range(len(A_GROUPS))]
    merged = _merge_prompt(z, attn, mkv.reshape(bsz, MEM_LEN, 2 * D_BRANCH), lw)
    x2 = _out_proj(merged.reshape(rows, MIX_WIDTH), lw["w_out"], x.reshape(rows, D_MODEL),
                   lw["g_post_mix"], 256)
    x3, tail = _ffn(x2, lw, tm, shift=1, prefix_rows=SUBLANES, blocks_per_seq=n // tm)
    tail = tail.reshape(bsz, n // tm, SUBLANES, -1)[:, -1]
    conv_new = _ff_untile(tail[:, SUBLANES - (CONV_W - 1):])
    a_new = [_kv_rows(z, g)[:, n - min(win, n):] for g, (win, _) in enumerate(A_GROUPS)]
    pool_new = z[:, n - C_PREFIX:, 5 * D_BRANCH:6 * D_BRANCH]
    mkv_new = mkv.reshape(bsz, MEM_LEN, 2, M_HEADS, GROUP_DIM)
    return x3.reshape(bsz, n, D_MODEL), a_new, mkv_new, pool_new, conv_new


def _sample_layer(x, caches, mem_kv, pool, conv, lw, tabs):
    rows = x.shape[0]
    bsz = SAMPLE_B
    z = _rms_matmul(x, lw["g_pre_mix"], lw["w_in"], rows, IN_COLS // 6)
    zt = z.reshape(SAMPLE_T, bsz, IN_COLS)
    zb = jnp.transpose(zt, (1, 0, 2))
    pad = ((0, 0), (0, T_PAD - SAMPLE_T), (0, 0))
    qkv = jnp.pad(zb[..., :3 * D_BRANCH], pad)
    qm = jnp.pad(zb[..., 6 * D_BRANCH:7 * D_BRANCH], pad)
    period = 2 * A_HPG * A_GROUPS[2][1]
    c0 = caches[0].reshape(bsz, -1, LANES)
    c1 = caches[1].reshape(bsz, -1, LANES)
    c2 = caches[2].reshape(bsz, -1, period, LANES)
    kept = 2 * A_HPG * SAMPLE_T
    biases = [_sample_bias(tabs[0], 0, c0.shape[1], 1, 1),
              _sample_bias(tabs[1], 1, c1.shape[1], 1, 1),
              _sample_bias(tabs[2], 2, c2.shape[1] * kept, SAMPLE_T, A_GROUPS[2][1])]
    bias_new = jnp.stack([_sample_bias_new(tabs[g], g) for g in range(len(A_GROUPS))])
    ya, ym = _attn_sample(qkv, qm, c0, c1, c2, mem_kv.reshape(bsz, MEM_LEN, 2 * D_BRANCH),
                          biases, bias_new)
    ya = jnp.transpose(ya[:, :SAMPLE_T], (1, 0, 2))
    ym = jnp.transpose(ym[:, :SAMPLE_T], (1, 0, 2))
    merged, v_new = _merge_sample(zt, ya, ym, jnp.transpose(pool, (1, 0, 2)), lw, 32)
    x2 = _out_proj(merged.reshape(rows, MIX_WIDTH), lw["w_out"], x, lw["g_post_mix"], 256)
    prefix = _ff_tiles(jnp.transpose(conv, (1, 0, 2)).reshape((CONV_W - 1) * bsz, 2 * D_FF), 1)
    x3, tail = _ffn(x2, lw, rows, shift=bsz, prefix_rows=(CONV_W - 1) * bsz, prefix=prefix)
    conv_new = jnp.transpose(_ff_untile(tail).reshape(CONV_W - 1, bsz, 2 * D_FF), (1, 0, 2))
    a_new = [_kv_rows(zb, g) for g in range(len(A_GROUPS))]
    xc_new = zb[..., 5 * D_BRANCH:6 * D_BRANCH]
    pool_new = jnp.concatenate([pool[:, SAMPLE_T:], xc_new], axis=1)
    return x3, a_new, jnp.transpose(v_new, (1, 0, 2)), pool_new, conv_new


def kernel(x_prompt, x_sample, cache_a_w128, cache_a_w512, cache_a_w2048, cache_mem_kv, state_pool, state_conv, mem_prompt, rel_bias, norm_pre_mix, w_in, norm_v_b, w_spatial, b_spatial, w_pool, pool_scale, norm_mem, w_mem_kv, w_out, norm_post_mix, norm_pre_ffn, w_up, conv_w, conv_b, w_down, norm_post_ffn):
    p = dict(norm_pre_mix=norm_pre_mix, w_in=w_in, norm_v_b=norm_v_b, w_spatial=w_spatial,
             b_spatial=b_spatial, w_pool=w_pool, pool_scale=pool_scale, norm_mem=norm_mem,
             w_mem_kv=w_mem_kv, w_out=w_out, norm_post_mix=norm_post_mix, norm_pre_ffn=norm_pre_ffn,
             w_up=w_up, conv_w=conv_w, conv_b=conv_b, w_down=w_down, norm_post_ffn=norm_post_ffn)
    tabs = _bias_tables(rel_bias)
    xp = x_prompt
    xs = jnp.transpose(x_sample, (1, 0, 2)).reshape(SAMPLE_T * SAMPLE_B, D_MODEL)
    outs_p = [[] for _ in range(6)]
    outs_s = [[] for _ in range(6)]
    for l in range(DEPTH):
        lw = _layer_weights(l, p)
        xp, a_new, mkv_new, pool_new, conv_new = _prompt_layer(xp, mem_prompt, lw, tabs)
        for slot, val in enumerate(a_new + [mkv_new, pool_new, conv_new]):
            outs_p[slot].append(val)
        xs, a_new, v_new, pool_new, conv_new = _sample_layer(
            xs, (cache_a_w128[l], cache_a_w512[l], cache_a_w2048[l]), cache_mem_kv[l],
            state_pool[l], state_conv[l], lw, tabs)
        for slot, val in enumerate(a_new + [v_new, pool_new, conv_new]):
            outs_s[slot].append(val)
    ys = jnp.transpose(xs.reshape(SAMPLE_T, SAMPLE_B, D_MODEL), (1, 0, 2))
    return (xp, ys) + tuple(jnp.stack(o) for o in outs_p) + tuple(jnp.stack(o) for o in outs_s)
```
